```python
import jax, jax.numpy as jnp
from jax import lax
import numpy as np

D_MODEL = 1024
BATCH = 32
SEQ = 2048
DEPTH = 4
DEC_BATCH = 2
DEC_SEQ = 8192
PAST_LEN = 128

D_A = 1024
A_CHUNK = 128
A_GROUPS = 8
A_GROUP_DIM = D_A // A_GROUPS
D_B = 1024
B_HEAD_DIM = 128
B_HEADS = D_B // B_HEAD_DIM
B_CHUNK = 16
N_EXPERTS = 16
EXPERT_FF = 1024
CAPACITY_FACTOR = 2
EPS = 1e-6
SPLIT_SIZES = (D_A, D_A, D_B, D_B, D_B, D_B, D_B, D_MODEL, D_MODEL)
D_IN = 2 * D_A + 5 * D_B + 2 * D_MODEL

kernel_name = "hybrid_gmlp_hgrn2_ec_encoder"


def rmsnorm(x, g):
    xf = x.astype(jnp.float32)
    y = xf * lax.rsqrt(jnp.mean(xf * xf, axis=-1, keepdims=True) + EPS)
    return (y * g.astype(jnp.float32)).astype(x.dtype)


def split_cols(z):
    out = []
    start = 0
    for s in SPLIT_SIZES:
        out.append(z[..., start:start + s])
        start += s
    return out


def lower_bounds(lb_logits):
    p = jax.nn.softmax(lb_logits.astype(jnp.float32), axis=1)
    c = jnp.cumsum(p, axis=1)
    return c - c[:, :1]


def gmlp_branch(u, v, vnorm_g, w_s, b_s):
    Bn, L, _ = v.shape
    n = L // A_CHUNK
    v = rmsnorm(v, vnorm_g)
    vc = v.reshape(Bn, n, A_CHUNK, A_GROUPS, A_GROUP_DIM)
    s = jnp.einsum('gts,bnsgc->bntgc', w_s, vc) + b_s.T[None, None, :, :, None]
    return u * s.reshape(Bn, L, D_A)


def chunk_gla(q, k, v, g):
    Bn, H, L, K = q.shape
    V = v.shape[-1]
    n = L // B_CHUNK

    def to_chunks(t):
        return t.reshape(Bn, H, n, B_CHUNK, t.shape[-1]).transpose(2, 0, 1, 3, 4)

    tri = jnp.tril(jnp.ones((B_CHUNK, B_CHUNK), dtype=bool))[:, :, None]

    def step(S, inp):
        qt, kt, vt, gt = inp
        G = jnp.cumsum(gt, axis=2)
        G_last = G[:, :, -1:, :]
        o_inter = jnp.einsum('bhtk,bhkv->bhtv', qt * jnp.exp(G), S)
        D = G[:, :, :, None, :] - G[:, :, None, :, :]
        decay = jnp.exp(jnp.where(tri, D, -jnp.inf))
        A = jnp.einsum('bhtk,bhtsk->bhts', qt, decay * kt[:, :, None, :, :])
        o = o_inter + jnp.einsum('bhts,bhsv->bhtv', A, vt)
        S = jnp.exp(G_last[:, :, 0, :])[..., None] * S + jnp.einsum(
            'bhsk,bhsv->bhkv', kt * jnp.exp(G_last - G), vt)
        return S, o

    S0 = jnp.zeros((Bn, H, K, V), jnp.float32)
    _, o = lax.scan(step, S0, (to_chunks(q), to_chunks(k), to_chunks(v), to_chunks(g)))
    return o.transpose(1, 2, 0, 3, 4).reshape(Bn, H, L, V)


def hgrn2_branch(q, i, z_fwd, z_bwd, lb, onorm_g, g_out):
    Bn, L, _ = q.shape

    def heads(t):
        return t.astype(jnp.float32).reshape(Bn, L, B_HEADS, B_HEAD_DIM).transpose(0, 2, 1, 3)

    def gates(z, lb_dir):
        lbh = lb_dir.reshape(1, B_HEADS, 1, B_HEAD_DIM)
        zh = heads(z)
        log_f = jnp.logaddexp(jnp.log(lbh), jnp.log1p(-lbh) + jax.nn.log_sigmoid(zh))
        k = (1.0 - lbh) * jax.nn.sigmoid(-zh)
        return k, log_f

    def rev(t):
        return jnp.flip(t, axis=2)

    qh, vh = heads(q), heads(i)
    k_f, g_f = gates(z_fwd, lb[0])
    k_b, g_b = gates(z_bwd, lb[1])
    o = chunk_gla(jnp.concatenate([qh, rev(qh)], axis=1),
                  jnp.concatenate([k_f, rev(k_b)], axis=1),
                  jnp.concatenate([vh, rev(vh)], axis=1),
                  jnp.concatenate([g_f, rev(g_b)], axis=1))
    o = o[:, :B_HEADS] + rev(o[:, B_HEADS:])
    o = o * lax.rsqrt(jnp.mean(o * o, axis=-1, keepdims=True) + EPS) * onorm_g.astype(jnp.float32)
    o = o.transpose(0, 2, 1, 3).reshape(Bn, L, D_B).astype(g_out.dtype)
    return o * jax.nn.silu(g_out)


def expert_choice_ffn(x, w_r, w_g, w_u, w_d):
    Bn, L, D = x.shape
    T = Bn * L
    C = CAPACITY_FACTOR * T // N_EXPERTS
    xt = x.reshape(T, D)
    aff = jax.nn.softmax(jnp.einsum('td,de->te', xt, w_r).astype(jnp.float32), axis=-1)
    gate, idx = lax.top_k(aff.T, C)
    xe = xt[idx]
    h = jax.nn.silu(jnp.einsum('ecd,edf->ecf', xe, w_g)) * jnp.einsum('ecd,edf->ecf', xe, w_u)
    ye = jnp.einsum('ecf,efd->ecd', h, w_d) * gate[..., None].astype(x.dtype)
    y = jnp.zeros_like(xt).at[idx.reshape(-1)].add(ye.reshape(-1, D))
    return y.reshape(Bn, L, D)


def trunk(x, norm1_g, w_in, a_vnorm_g, a_spatial_w, a_spatial_b, b_lb_logits, b_onorm_g,
          w_branch_a, w_branch_b, w_out, norm2_g, w_router, w_exp_gate, w_exp_up,
          w_exp_down, final_g):
    lbs = lower_bounds(b_lb_logits)
    for l in range(DEPTH):
        h = rmsnorm(x, norm1_g[l])
        z = jnp.einsum('bld,de->ble', h, w_in[l])
        u, v, q, i, z_f, z_b, g_o, g_a, g_b = split_cols(z)
        ya = gmlp_branch(jax.nn.gelu(u), jax.nn.gelu(v), a_vnorm_g[l], a_spatial_w[l], a_spatial_b[l])
        yb = hgrn2_branch(q, i, z_f, z_b, lbs[:, l], b_onorm_g[l], g_o)
        m = (jax.nn.sigmoid(g_a) * jnp.einsum('bla,ad->bld', ya, w_branch_a[l])
             + jax.nn.sigmoid(g_b) * jnp.einsum('blc,cd->bld', yb, w_branch_b[l]))
        x = x + jnp.einsum('bld,de->ble', m, w_out[l])
        x = x + expert_choice_ffn(rmsnorm(x, norm2_g[l]), w_router[l], w_exp_gate[l],
                                  w_exp_up[l], w_exp_down[l])
    return rmsnorm(x, final_g)


def setup_inputs(seed: int = 0) -> dict:
    key = jax.random.key(seed)
    ks = jax.random.split(key, 20)
    f32 = jnp.float32

    def nrm(k, shape, scale):
        return jax.random.normal(k, shape, f32) * scale

    return {
        'x_prompt': nrm(ks[0], (BATCH, SEQ, D_MODEL), 1.0),
        'x_sample': nrm(ks[1], (DEC_BATCH, DEC_SEQ, D_MODEL), 1.0),
        'norm1_g': 1.0 + nrm(ks[2], (DEPTH, D_MODEL), 0.02),
        'w_in': nrm(ks[3], (DEPTH, D_MODEL, D_IN), D_MODEL ** -0.5),
        'a_vnorm_g': 1.0 + nrm(ks[4], (DEPTH, D_A), 0.02),
        'a_spatial_w': nrm(ks[5], (DEPTH, A_GROUPS, A_CHUNK, A_CHUNK), A_CHUNK ** -0.5),
        'a_spatial_b': 1.0 + nrm(ks[6], (DEPTH, A_GROUPS, A_CHUNK), 0.02),
        'b_lb_logits': 1.0 + nrm(ks[7], (2, DEPTH, D_B), 0.1),
        'b_onorm_g': 1.0 + nrm(ks[8], (DEPTH, B_HEAD_DIM), 0.02),
        'w_branch_a': nrm(ks[9], (DEPTH, D_A, D_MODEL), D_A ** -0.5),
        'w_branch_b': nrm(ks[10], (DEPTH, D_B, D_MODEL), D_B ** -0.5),
        'w_out': nrm(ks[11], (DEPTH, D_MODEL, D_MODEL), D_MODEL ** -0.5),
        'norm2_g': 1.0 + nrm(ks[12], (DEPTH, D_MODEL), 0.02),
        'w_router': nrm(ks[13], (DEPTH, D_MODEL, N_EXPERTS), D_MODEL ** -0.5),
        'w_exp_gate': nrm(ks[14], (DEPTH, N_EXPERTS, D_MODEL, EXPERT_FF), D_MODEL ** -0.5),
        'w_exp_up': nrm(ks[15], (DEPTH, N_EXPERTS, D_MODEL, EXPERT_FF), D_MODEL ** -0.5),
        'w_exp_down': nrm(ks[16], (DEPTH, N_EXPERTS, EXPERT_FF, D_MODEL), EXPERT_FF ** -0.5),
        'final_g': 1.0 + nrm(ks[17], (D_MODEL,), 0.02),
    }


def reference(x_prompt, x_sample, norm1_g, w_in, a_vnorm_g, a_spatial_w, a_spatial_b,
              b_lb_logits, b_onorm_g, w_branch_a, w_branch_b, w_out, norm2_g, w_router,
              w_exp_gate, w_exp_up, w_exp_down, final_g):
    y_prompt = trunk(x_prompt, norm1_g, w_in, a_vnorm_g, a_spatial_w, a_spatial_b, b_lb_logits,
                     b_onorm_g, w_branch_a, w_branch_b, w_out, norm2_g, w_router, w_exp_gate,
                     w_exp_up, w_exp_down, final_g)
    y_sample = trunk(x_sample, norm1_g, w_in, a_vnorm_g, a_spatial_w, a_spatial_b, b_lb_logits,
                     b_onorm_g, w_branch_a, w_branch_b, w_out, norm2_g, w_router, w_exp_gate,
                     w_exp_up, w_exp_down, final_g)
    return (y_prompt, y_sample)
```

```python
import functools

import jax
import jax.numpy as jnp
from jax import lax
from jax.experimental import pallas as pl
from jax.experimental.pallas import tpu as pltpu

F32 = jnp.float32
BF16 = jnp.bfloat16
I32 = jnp.int32

D_MODEL = 1024
N_SPLIT = 9
HEAD_DIM = 128
N_HEADS = 8
SP_CHUNK = 128
SP_GROUPS = 8
N_EXPERTS = 16
CAPACITY_FACTOR = 2
EPS = 1e-6

LANES = 128
HG_CHUNK = 128
HG_UNSAFE = -150.0
TOK_CHUNK = 128
ROW_ALIGN = 16
WIN_FAST = 48
WIN_BIG = TOK_CHUNK + ROW_ALIGN
XG_COLS = D_MODEL + LANES
VMEM_LIMIT = 56 * 1024 * 1024


def _cparams(sem):
    return pltpu.CompilerParams(dimension_semantics=sem, vmem_limit_bytes=VMEM_LIMIT)


def _sigmoid(x):
    return 1.0 / (1.0 + jnp.exp(-x))


def _gelu_tanh(x):
    return x * (0.5 * (1.0 + jnp.tanh(0.7978845608028654 * (x + 0.044715 * (x * x * x)))))


def _rmsnorm_rows(x, g):
    ms = jnp.mean(x * x, axis=-1, keepdims=True)
    return x * lax.rsqrt(ms + EPS) * g


def _lb_body(logit_ref, o_ref):
    x = logit_ref[...]
    m = jnp.max(x, axis=1, keepdims=True)
    e = jnp.exp(x - m)
    p = e / jnp.sum(e, axis=1, keepdims=True)
    depth = x.shape[1]
    acc = jnp.zeros_like(p[:, 0:1, :])
    for l in range(depth):
        o_ref[:, l:l + 1, :] = acc
        if l + 1 < depth:
            acc = acc + p[:, l + 1:l + 2, :]


def _lower_bounds(lb_logits):
    return pl.pallas_call(
        _lb_body, out_shape=jax.ShapeDtypeStruct(lb_logits.shape, F32), name="lower_bounds",
    )(lb_logits.astype(F32))


def _inproj_body(x_ref, g1_ref, w_ref, vg_ref, o_ref, xn_ref):
    j = pl.program_id(1)

    @pl.when(j == 0)
    def _():
        xn_ref[...] = _rmsnorm_rows(x_ref[...], g1_ref[...]).astype(BF16)

    z = jnp.dot(xn_ref[...], w_ref[...], preferred_element_type=F32)

    @pl.when(j == 0)
    def _():
        o_ref[...] = _gelu_tanh(z).astype(BF16)

    @pl.when(j == 1)
    def _():
        o_ref[...] = _rmsnorm_rows(_gelu_tanh(z), vg_ref[...]).astype(BF16)

    @pl.when(jnp.logical_and(j >= 2, j <= 5))
    def _():
        o_ref[...] = z.astype(BF16)

    @pl.when(j == 6)
    def _():
        o_ref[...] = (z * _sigmoid(z)).astype(BF16)

    @pl.when(j >= 7)
    def _():
        o_ref[...] = _sigmoid(z).astype(BF16)


def _inproj(x, g1, w_in, vg):
    T = x.shape[0]
    tm = min(1024, T)
    return pl.pallas_call(
        _inproj_body,
        grid=(T // tm, N_SPLIT),
        in_specs=[
            pl.BlockSpec((tm, D_MODEL), lambda i, j: (i, 0)),
            pl.BlockSpec((1, D_MODEL), lambda i, j: (0, 0)),
            pl.BlockSpec((D_MODEL, D_MODEL), lambda i, j: (0, j)),
            pl.BlockSpec((1, D_MODEL), lambda i, j: (0, 0)),
        ],
        out_specs=pl.BlockSpec((tm, D_MODEL), lambda i, j: (i, j)),
        out_shape=jax.ShapeDtypeStruct((T, N_SPLIT * D_MODEL), BF16),
        scratch_shapes=[pltpu.VMEM((tm, D_MODEL), BF16)],
        compiler_params=_cparams(("parallel", "arbitrary")),
        name="inproj",
    )(x, g1, w_in, vg)


def _hgrn_body(q_ref, v_ref, zf_ref, zb_ref, go_ref, lb_ref, og_ref, o_ref,
               acc_ref, gs_ref, ks_ref, vs_ref, *, L):
    C = HG_CHUNK
    n = L // C
    rid = lax.broadcasted_iota(I32, (C, C), 0)
    cid = lax.broadcasted_iota(I32, (C, C), 1)
    rowi = lax.broadcasted_iota(I32, (C, 1), 0)
    og = og_ref[...]
    nt = (((1,), (1,)), ((), ()))

    def direction(z_ref, lbrow, reverse):
        lb = lb_ref[lbrow:lbrow + 1, :]
        log_lb = jnp.log(lb)
        log1m = jnp.log1p(-lb)
        onem = 1.0 - lb
        mask = (cid >= rid) if reverse else (cid <= rid)
        tri = jnp.where(mask, 1.0, 0.0).astype(BF16)

        def chunk(ci, st):
            c = (n - 1 - ci) if reverse else ci
            sl = pl.ds(pl.multiple_of(c * C, C), C)
            z = z_ref[sl, :].astype(F32)
            q = q_ref[sl, :].astype(F32)
            vb = v_ref[sl, :]
            ls = jnp.minimum(z, 0.0) - jnp.log1p(jnp.exp(-jnp.abs(z)))
            b = log1m + ls
            g = jnp.maximum(log_lb, b) + jnp.log1p(jnp.exp(-jnp.abs(log_lb - b)))
            kk = onem * (1.0 / (1.0 + jnp.exp(z)))
            g_hi = g.astype(BF16)
            g_lo = (g - g_hi.astype(F32)).astype(BF16)
            G = (jnp.dot(tri, g_hi, preferred_element_type=F32)
                 + jnp.dot(tri, g_lo, preferred_element_type=F32))
            gtot = G[0:1, :] if reverse else G[C - 1:C, :]
            qi = (q * jnp.exp(G)).astype(BF16)
            o_inter = lax.dot_general(qi, st.astype(BF16), nt, preferred_element_type=F32)

            def fast():
                gref = 0.5 * gtot
                qa = (q * jnp.exp(G - gref)).astype(BF16)
                ka = (kk * jnp.exp(gref - G)).astype(BF16)
                a = lax.dot_general(qa, ka, nt, preferred_element_type=F32)
                a = jnp.where(mask, a, 0.0).astype(BF16)
                return jnp.dot(a, vb, preferred_element_type=F32)

            def slow():
                gs_ref[...] = G
                ks_ref[...] = kk
                vs_ref[...] = vb.astype(F32)

                def sbody(s, o):
                    gs = gs_ref[pl.ds(s, 1), :]
                    dec = jnp.exp(jnp.minimum(G - gs, 0.0))
                    w = jnp.sum(q * ks_ref[pl.ds(s, 1), :] * dec, axis=1, keepdims=True)
                    keep = (rowi <= s) if reverse else (rowi >= s)
                    return o + jnp.where(keep, w, 0.0) * vs_ref[pl.ds(s, 1), :]

                return lax.fori_loop(0, C, sbody, jnp.zeros((C, HEAD_DIM), F32))

            o = o_inter + lax.cond(jnp.min(gtot) < HG_UNSAFE, slow, fast)
            kst = (kk * jnp.exp(gtot - G)).astype(BF16)
            vt = vb.astype(F32).T.astype(BF16)
            st_new = jnp.exp(gtot) * st + jnp.dot(vt, kst, preferred_element_type=F32)
            if reverse:
                tot = acc_ref[sl, :] + o
                ms = jnp.mean(tot * tot, axis=-1, keepdims=True)
                y = tot * lax.rsqrt(ms + EPS) * og
                o_ref[sl, :] = (y * go_ref[sl, :].astype(F32)).astype(o_ref.dtype)
            else:
                acc_ref[sl, :] = o
            return st_new

        lax.fori_loop(0, n, chunk, jnp.zeros((HEAD_DIM, HEAD_DIM), F32))

    direction(zf_ref, 0, False)
    direction(zb_ref, 1, True)


def _hgrn(z3, lb, og):
    B, L, _ = z3.shape
    cb = D_MODEL // HEAD_DIM

    def col(split):
        return pl.BlockSpec((None, L, HEAD_DIM), lambda b, h: (b, 0, split * cb + h))

    return pl.pallas_call(
        functools.partial(_hgrn_body, L=L),
        grid=(B, N_HEADS),
        in_specs=[col(2), col(3), col(4), col(5), col(6),
                  pl.BlockSpec((2, HEAD_DIM), lambda b, h: (0, h)),
                  pl.BlockSpec((1, HEAD_DIM), lambda b, h: (0, 0))],
        out_specs=pl.BlockSpec((None, L, HEAD_DIM), lambda b, h: (b, 0, h)),
        out_shape=jax.ShapeDtypeStruct((B, L, D_MODEL), BF16),
        scratch_shapes=[pltpu.VMEM((L, HEAD_DIM), F32),
                        pltpu.VMEM((HG_CHUNK, HEAD_DIM), F32),
                        pltpu.VMEM((HG_CHUNK, HEAD_DIM), F32),
                        pltpu.VMEM((HG_CHUNK, HEAD_DIM), F32)],
        compiler_params=_cparams(("parallel", "parallel")),
        name="hgrn",
    )(z3, z3, z3, z3, z3, lb, og)


def _mix_body(u_ref, vn_ref, ga_ref, gb_ref, yb_ref, x_ref, ws_ref, bs_ref, wa_ref, wb_ref,
              wo_ref, g2_ref, wr_ref, wrt_ref, xo_ref, xg_ref, afft_ref, ya_ref, *, tm):
    gw = D_MODEL // SP_GROUPS
    for ci in range(tm // SP_CHUNK):
        rs = slice(ci * SP_CHUNK, (ci + 1) * SP_CHUNK)
        for g in range(SP_GROUPS):
            cs = slice(g * gw, (g + 1) * gw)
            s = jnp.dot(ws_ref[g], vn_ref[rs, cs], preferred_element_type=F32) + bs_ref[:, g:g + 1]
            ya_ref[rs, cs] = (u_ref[rs, cs].astype(F32) * s).astype(BF16)
    pa = jnp.dot(ya_ref[...], wa_ref[...], preferred_element_type=F32)
    pb = jnp.dot(yb_ref[...], wb_ref[...], preferred_element_type=F32)
    m = ga_ref[...].astype(F32) * pa + gb_ref[...].astype(F32) * pb
    xo = x_ref[...] + jnp.dot(m.astype(BF16), wo_ref[...], preferred_element_type=F32)
    xo_ref[...] = xo
    xn = _rmsnorm_rows(xo, g2_ref[...])
    hp = lax.Precision.HIGHEST
    lane = lax.broadcasted_iota(I32, (tm, LANES), 1)
    first = lane < N_EXPERTS
    lg = jnp.dot(xn, wr_ref[...], precision=hp, preferred_element_type=F32)
    e1 = jnp.exp(lg - jnp.max(jnp.where(first, lg, -jnp.inf), axis=1, keepdims=True))
    aff = e1 / jnp.sum(jnp.where(first, e1, 0.0), axis=1, keepdims=True)
    lgt = lax.dot_general(wrt_ref[...], xn, (((1,), (1,)), ((), ())), precision=hp,
                          preferred_element_type=F32)
    e2 = jnp.exp(lgt - jnp.max(lgt, axis=0, keepdims=True))
    afft_ref[...] = e2 / jnp.sum(e2, axis=0, keepdims=True)
    p0 = aff.astype(BF16).astype(F32)
    r1 = aff - p0
    p1 = r1.astype(BF16).astype(F32)
    p2 = r1 - p1
    pieces = jnp.where(first, p0, jnp.where(lane < 2 * N_EXPERTS, p1,
                                            jnp.where(lane < 3 * N_EXPERTS, p2, 0.0)))
    xg_ref[:, :D_MODEL] = xn.astype(BF16)
    xg_ref[:, D_MODEL:] = pieces.astype(BF16)


def _mix(z, yb, x, ws, bs_t, wa, wb, wo, g2, wr, wrt):
    T = x.shape[0]
    tm = min(256, T)
    row = lambda j: pl.BlockSpec((tm, D_MODEL), lambda i, j=j: (i, j))
    full = lambda shp: pl.BlockSpec(shp, lambda i: (0,) * len(shp))
    return pl.pallas_call(
        functools.partial(_mix_body, tm=tm),
        grid=(T // tm,),
        in_specs=[row(0), row(1), row(7), row(8), row(0), row(0),
                  full((SP_GROUPS, SP_CHUNK, SP_CHUNK)), full((SP_CHUNK, SP_GROUPS)),
                  full((D_MODEL, D_MODEL)), full((D_MODEL, D_MODEL)), full((D_MODEL, D_MODEL)),
                  full((1, D_MODEL)), full((D_MODEL, LANES)), full((N_EXPERTS, D_MODEL))],
        out_specs=[pl.BlockSpec((tm, D_MODEL), lambda i: (i, 0)),
                   pl.BlockSpec((tm, XG_COLS), lambda i: (i, 0)),
                   pl.BlockSpec((N_EXPERTS, tm), lambda i: (0, i))],
        out_shape=[jax.ShapeDtypeStruct((T, D_MODEL), F32),
                   jax.ShapeDtypeStruct((T, XG_COLS), BF16),
                   jax.ShapeDtypeStruct((N_EXPERTS, T), F32)],
        scratch_shapes=[pltpu.VMEM((tm, D_MODEL), BF16)],
        compiler_params=_cparams(("parallel",)),
        name="mix",
    )(z, z, z, z, yb, x, ws, bs_t, wa, wb, wo, g2, wr, wrt)


def _select_body(aff_ref, posm_ref, offs_ref, run_ref, *, T, cap):
    E = N_EXPERTS
    CH = TOK_CHUNK
    nc = T // CH
    ncp = offs_ref.shape[1]
    BL = min(T, 2048)
    nb = T // BL

    def bits(sl):
        return lax.bitcast_convert_type(aff_ref[:, sl], I32)

    def count_ge(cand):
        def blk(i, acc):
            b = bits(pl.ds(pl.multiple_of(i * BL, BL), BL))
            return acc + jnp.where(b >= cand, 1, 0).astype(I32)
        acc = lax.fori_loop(0, nb, blk, jnp.zeros((E, BL), I32))
        return jnp.sum(acc, axis=1, keepdims=True)

    def rbody(i, prefix):
        cand = prefix | lax.shift_left(jnp.int32(1), 30 - i)
        return jnp.where(count_ge(cand) >= cap, cand, prefix)

    thr = lax.fori_loop(0, 31, rbody, jnp.zeros((E, 1), I32))
    need = (cap - count_ge(thr + 1)).astype(F32)

    r = lax.broadcasted_iota(I32, (CH, CH), 0)
    c = lax.broadcasted_iota(I32, (CH, CH), 1)
    upper = jnp.where(r <= c, 1.0, 0.0).astype(BF16)
    lane = lax.broadcasted_iota(I32, (E, ncp), 1)

    run_ref[...] = jnp.zeros_like(run_ref)
    offs_ref[...] = jnp.zeros_like(offs_ref)

    def cbody(ci, carry):
        eq_run = run_ref[0][:, 0:1]
        sel_run = run_ref[1][:, 0:1]
        sl = pl.ds(pl.multiple_of(ci * CH, CH), CH)
        b = bits(sl)
        gt = b > thr
        eq = b == thr
        eqf = jnp.where(eq, 1.0, 0.0)
        eqcs = jnp.dot(eqf.astype(BF16), upper, preferred_element_type=F32)
        sel = jnp.logical_or(gt, jnp.logical_and(eq, (eq_run + eqcs - eqf) < need))
        self_ = jnp.where(sel, 1.0, 0.0)
        selcs = jnp.dot(self_.astype(BF16), upper, preferred_element_type=F32)
        pos = sel_run + selcs - self_
        posm_ref[:, sl] = jnp.where(sel, pos, -1.0).astype(I32)
        offs_ref[...] = jnp.where(lane == ci, sel_run.astype(I32), offs_ref[...])
        run_ref[0] = jnp.broadcast_to(eq_run + eqcs[:, CH - 1:CH], (E, LANES))
        run_ref[1] = jnp.broadcast_to(sel_run + selcs[:, CH - 1:CH], (E, LANES))
        return carry

    lax.fori_loop(0, nc, cbody, 0)
    offs_ref[...] = jnp.where(lane == nc, run_ref[1][:, 0:1].astype(I32), offs_ref[...])


def _select(afft, cap):
    E, T = afft.shape
    nc = T // TOK_CHUNK
    ncp = ((nc + 1 + LANES - 1) // LANES) * LANES
    return pl.pallas_call(
        functools.partial(_select_body, T=T, cap=cap),
        out_shape=[jax.ShapeDtypeStruct((E, T), I32), jax.ShapeDtypeStruct((E, ncp), I32)],
        scratch_shapes=[pltpu.VMEM((2, E, LANES), F32)],
        compiler_params=pltpu.CompilerParams(vmem_limit_bytes=VMEM_LIMIT),
        name="select",
    )(afft)


def _chunk_scalars(offs_ref, c):
    off = [offs_ref[e, c] for e in range(N_EXPERTS)]
    cnt = [offs_ref[e, c + 1] - off[e] for e in range(N_EXPERTS)]
    nmax = cnt[0]
    for e in range(1, N_EXPERTS):
        nmax = jnp.maximum(nmax, cnt[e])
    return off, cnt, nmax <= WIN_FAST - ROW_ALIGN


def _column(vals):
    rowid = lax.broadcasted_iota(I32, (N_EXPERTS, 1), 0)
    col = jnp.zeros((N_EXPERTS, 1), I32)
    for e, v in enumerate(vals):
        col = jnp.where(rowid == e, v, col)
    return col


def _align_down(v):
    return v & (-ROW_ALIGN)


def _gather_body(offs_ref, xg_ref, posm_ref, xe_ref, carry_ref, mer_ref, stage_ref, big_ref,
                 pend_ref, sem, *, nc, cap):
    E = N_EXPERTS
    c = pl.program_id(0)
    par = c & 1

    @pl.when(c == 0)
    def _():
        carry_ref[...] = jnp.zeros_like(carry_ref)
        pend_ref[0] = 0
        big_ref[0] = jnp.zeros((WIN_BIG, XG_COLS), BF16)
        for e in range(E):
            pltpu.make_async_copy(big_ref.at[0], xe_ref.at[e, pl.ds(cap, WIN_BIG), :], sem.at[2]).start()
        for e in range(E):
            pltpu.make_async_copy(big_ref.at[0], xe_ref.at[e, pl.ds(cap, WIN_BIG), :], sem.at[2]).wait()

    off, cnt, fast = _chunk_scalars(offs_ref, c)
    sa = [_align_down(off[e]) for e in range(E)]
    delta = [_align_down(off[e] + cnt[e]) - sa[e] for e in range(E)]
    posm = posm_ref[...]
    rr = jnp.where(posm >= 0, posm - _column(sa), -1)

    def onehot(e, rows):
        kio = lax.broadcasted_iota(I32, (rows, TOK_CHUNK), 0)
        return jnp.where(kio == rr[e:e + 1, :], 1.0, 0.0).astype(BF16)

    def merge(e, reg, rows, dst):
        mer_ref[0:ROW_ALIGN, :] = reg[0:ROW_ALIGN, :] + carry_ref[e]
        mer_ref[ROW_ALIGN:rows, :] = reg[ROW_ALIGN:rows, :]
        dst[...] = mer_ref[0:rows, :].astype(BF16)
        carry_ref[e] = mer_ref[pl.ds(pl.multiple_of(delta[e], ROW_ALIGN), ROW_ALIGN), :]

    def window(e, rows):
        return xe_ref.at[e, pl.ds(pl.multiple_of(sa[e], ROW_ALIGN), rows), :]

    def wait_pending():
        @pl.when(pend_ref[0] == 1)
        def _():
            for e in range(E):
                pltpu.make_async_copy(stage_ref.at[1 - par, e], window(e, WIN_FAST),
                                      sem.at[1 - par]).wait()
            pend_ref[0] = 0

    @pl.when(fast)
    def _():
        p = jnp.concatenate([onehot(e, WIN_FAST) for e in range(E)], axis=0)
        res = jnp.dot(p, xg_ref[...], preferred_element_type=F32)
        for e in range(E):
            merge(e, res[e * WIN_FAST:(e + 1) * WIN_FAST, :], WIN_FAST, stage_ref.at[par, e])
        wait_pending()
        for e in range(E):
            pltpu.make_async_copy(stage_ref.at[par, e], window(e, WIN_FAST), sem.at[par]).start()
        pend_ref[0] = 1

    @pl.when(jnp.logical_not(fast))
    def _():
        wait_pending()
        for e in range(E):
            res = jnp.dot(onehot(e, WIN_BIG), xg_ref[...], preferred_element_type=F32)
            merge(e, res, WIN_BIG, big_ref.at[e])
        for e in range(E):
            pltpu.make_async_copy(big_ref.at[e], window(e, WIN_BIG), sem.at[2]).start()
        for e in range(E):
            pltpu.make_async_copy(big_ref.at[e], window(e, WIN_BIG), sem.at[2]).wait()

    @pl.when(c == nc - 1)
    def _():
        @pl.when(pend_ref[0] == 1)
        def _():
            for e in range(E):
                pltpu.make_async_copy(stage_ref.at[par, e], window(e, WIN_FAST), sem.at[par]).wait()
            pend_ref[0] = 0


def _gather(offs, xg, posm, cap):
    T = xg.shape[0]
    nc = T // TOK_CHUNK
    E = N_EXPERTS
    grid_spec = pltpu.PrefetchScalarGridSpec(
        num_scalar_prefetch=1,
        grid=(nc,),
        in_specs=[pl.BlockSpec((TOK_CHUNK, XG_COLS), lambda c, o: (c, 0)),
                  pl.BlockSpec((E, TOK_CHUNK), lambda c, o: (0, c))],
        out_specs=pl.BlockSpec(memory_space=pl.ANY),
        scratch_shapes=[pltpu.VMEM((E, ROW_ALIGN, XG_COLS), F32),
                        pltpu.VMEM((WIN_BIG, XG_COLS), F32),
                        pltpu.VMEM((2, E, WIN_FAST, XG_COLS), BF16),
                        pltpu.VMEM((E, WIN_BIG, XG_COLS), BF16),
                        pltpu.SMEM((1,), I32),
                        pltpu.SemaphoreType.DMA((3,))],
    )
    return pl.pallas_call(
        functools.partial(_gather_body, nc=nc, cap=cap),
        grid_spec=grid_spec,
        out_shape=jax.ShapeDtypeStruct((E, cap + WIN_BIG, XG_COLS), BF16),
        compiler_params=_cparams(("arbitrary",)),
        name="gather",
    )(offs, xg, posm)


def _ffn_body(xe_ref, wg_ref, wu_ref, wd_ref, o_ref):
    e = pl.program_id(0)
    x = xe_ref[:, :D_MODEL]
    pieces = xe_ref[:, D_MODEL:].astype(F32)
    lane = lax.broadcasted_iota(I32, pieces.shape, 1)
    mine = jnp.logical_and((lane & (N_EXPERTS - 1)) == e, lane < 3 * N_EXPERTS)
    gate = jnp.sum(jnp.where(mine, pieces, 0.0), axis=1, keepdims=True)
    hg = jnp.dot(x, wg_ref[...], preferred_element_type=F32)
    hu = jnp.dot(x, wu_ref[...], preferred_element_type=F32)
    h = (hg * _sigmoid(hg) * hu).astype(BF16)
    y = jnp.dot(h, wd_ref[...], preferred_element_type=F32)
    o_ref[...] = (y * gate).astype(o_ref.dtype)


def _ffn(xe, wg, wu, wd, cap):
    E = N_EXPERTS
    tm = min(512, cap)
    ff = wg.shape[-1]
    return pl.pallas_call(
        _ffn_body,
        grid=(E, cap // tm),
        in_specs=[pl.BlockSpec((None, tm, XG_COLS), lambda e, i: (e, i, 0)),
                  pl.BlockSpec((None, D_MODEL, ff), lambda e, i: (e, 0, 0)),
                  pl.BlockSpec((None, D_MODEL, ff), lambda e, i: (e, 0, 0)),
                  pl.BlockSpec((None, ff, D_MODEL), lambda e, i: (e, 0, 0))],
        out_specs=pl.BlockSpec((None, tm, D_MODEL), lambda e, i: (e, i, 0)),
        out_shape=jax.ShapeDtypeStruct((E, cap, D_MODEL), BF16),
        compiler_params=_cparams(("parallel", "parallel")),
        name="ffn",
    )(xe, wg, wu, wd)


def _combine_body(offs_ref, posm_ref, x_ref, fg_ref, ye_ref, o_ref, ybuf_ref, ybig_ref, sem,
                  *, nc, cap, final_norm):
    E = N_EXPERTS
    c = pl.program_id(0)

    def starts(cc, rows):
        off, _, fast = _chunk_scalars(offs_ref, cc)
        return [jnp.minimum(_align_down(off[e]), cap - rows) for e in range(E)], fast

    def window(e, start, rows):
        return ye_ref.at[e, pl.ds(pl.multiple_of(start, ROW_ALIGN), rows), :]

    def fetch(cc):
        sa, fast = starts(cc, WIN_FAST)

        @pl.when(fast)
        def _():
            slot = cc & 1
            for e in range(E):
                pltpu.make_async_copy(window(e, sa[e], WIN_FAST),
                                      ybuf_ref.at[slot, pl.ds(e * WIN_FAST, WIN_FAST), :],
                                      sem.at[slot]).start()

    @pl.when(c == 0)
    def _():
        fetch(c)

    @pl.when(c + 1 < nc)
    def _():
        fetch(c + 1)

    posm = posm_ref[...]

    def unpack(sa, rows, y):
        rr = jnp.where(posm >= 0, posm - _column(sa), -1).astype(F32)
        rt = rr.T.astype(BF16)
        ecol = lax.broadcasted_iota(I32, (E, E * rows), 1) // rows
        erow = lax.broadcasted_iota(I32, (E, E * rows), 0)
        spread = jnp.where(ecol == erow, 1.0, 0.0).astype(BF16)
        rexp = jnp.dot(rt, spread, preferred_element_type=F32)
        jmod = (lax.broadcasted_iota(I32, (1, E * rows), 1) % rows).astype(F32)
        w = jnp.where(rexp == jmod, 1.0, 0.0).astype(BF16)
        return jnp.dot(w, y, preferred_element_type=F32)

    def finish(upd):
        xo = x_ref[...] + upd
        if final_norm:
            xo = _rmsnorm_rows(xo, fg_ref[...])
        o_ref[...] = xo

    sa_f, fast = starts(c, WIN_FAST)

    @pl.when(fast)
    def _():
        slot = c & 1
        for e in range(E):
            pltpu.make_async_copy(window(e, sa_f[e], WIN_FAST),
                                  ybuf_ref.at[slot, pl.ds(e * WIN_FAST, WIN_FAST), :],
                                  sem.at[slot]).wait()
        finish(unpack(sa_f, WIN_FAST, ybuf_ref[slot]))

    @pl.when(jnp.logical_not(fast))
    def _():
        sa_b, _ = starts(c, WIN_BIG)
        for e in range(E):
            pltpu.make_async_copy(window(e, sa_b[e], WIN_BIG),
                                  ybig_ref.at[pl.ds(e * WIN_BIG, WIN_BIG), :], sem.at[2]).start()
        for e in range(E):
            pltpu.make_async_copy(window(e, sa_b[e], WIN_BIG),
                                  ybig_ref.at[pl.ds(e * WIN_BIG, WIN_BIG), :], sem.at[2]).wait()
        finish(unpack(sa_b, WIN_BIG, ybig_ref[...]))


def _combine(offs, posm, x, fg, ye, cap, final_norm):
    T = x.shape[0]
    nc = T // TOK_CHUNK
    E = N_EXPERTS
    grid_spec = pltpu.PrefetchScalarGridSpec(
        num_scalar_prefetch=1,
        grid=(nc,),
        in_specs=[pl.BlockSpec((E, TOK_CHUNK), lambda c, o: (0, c)),
                  pl.BlockSpec((TOK_CHUNK, D_MODEL), lambda c, o: (c, 0)),
                  pl.BlockSpec((1, D_MODEL), lambda c, o: (0, 0)),
                  pl.BlockSpec(memory_space=pl.ANY)],
        out_specs=pl.BlockSpec((TOK_CHUNK, D_MODEL), lambda c, o: (c, 0)),
        scratch_shapes=[pltpu.VMEM((2, E * WIN_FAST, D_MODEL), BF16),
                        pltpu.VMEM((E * WIN_BIG, D_MODEL), BF16),
                        pltpu.SemaphoreType.DMA((3,))],
    )
    return pl.pallas_call(
        functools.partial(_combine_body, nc=nc, cap=cap, final_norm=final_norm),
        grid_spec=grid_spec,
        out_shape=jax.ShapeDtypeStruct((T, D_MODEL), F32),
        compiler_params=_cparams(("arbitrary",)),
        name="combine",
    )(offs, posm, x, fg, ye)


def _trunk(x, lbs, p):
    B, L, _ = x.shape
    T = B * L
    cap = CAPACITY_FACTOR * T // N_EXPERTS
    depth = p["w_in"].shape[0]
    assert L % HG_CHUNK == 0 and T % TOK_CHUNK == 0 and cap % ROW_ALIGN == 0 and cap >= WIN_BIG
    x = x.reshape(T, D_MODEL)
    for l in range(depth):
        z = _inproj(x, p["norm1_g"][l], p["w_in"][l], p["a_vnorm_g"][l])
        yb = _hgrn(z.reshape(B, L, N_SPLIT * D_MODEL), lbs[:, l], p["b_onorm_g"][l])
        x, xg, afft = _mix(z, yb.reshape(T, D_MODEL), x, p["a_spatial_w"][l], p["a_spatial_b_t"][l],
                           p["w_branch_a"][l], p["w_branch_b"][l], p["w_out"][l], p["norm2_g"][l],
                           p["w_router"][l], p["w_router_t"][l])
        posm, offs = _select(afft, cap)
        xe = _gather(offs, xg, posm, cap)
        ye = _ffn(xe, p["w_exp_gate"][l], p["w_exp_up"][l], p["w_exp_down"][l], cap)
        x = _combine(offs, posm, x, p["final_g"], ye, cap, final_norm=(l == depth - 1))
    return x.reshape(B, L, D_MODEL)


def kernel(x_prompt, x_sample, norm1_g, w_in, a_vnorm_g, a_spatial_w, a_spatial_b, b_lb_logits,
           b_onorm_g, w_branch_a, w_branch_b, w_out, norm2_g, w_router, w_exp_gate, w_exp_up,
           w_exp_down, final_g):
    depth = w_in.shape[0]
    p = {
        "norm1_g": norm1_g.astype(F32).reshape(depth, 1, D_MODEL),
        "w_in": w_in.astype(BF16),
        "a_vnorm_g": a_vnorm_g.astype(F32).reshape(depth, 1, D_MODEL),
        "a_spatial_w": a_spatial_w.astype(BF16),
        "a_spatial_b_t": jnp.swapaxes(a_spatial_b.astype(F32), 1, 2),
        "b_onorm_g": b_onorm_g.astype(F32).reshape(depth, 1, HEAD_DIM),
        "w_branch_a": w_branch_a.astype(BF16),
        "w_branch_b": w_branch_b.astype(BF16),
        "w_out": w_out.astype(BF16),
        "norm2_g": norm2_g.astype(F32).reshape(depth, 1, D_MODEL),
        "w_router": jnp.pad(jnp.tile(w_router.astype(F32), (1, 1, 3)),
                            ((0, 0), (0, 0), (0, LANES - 3 * N_EXPERTS))),
        "w_router_t": jnp.swapaxes(w_router.astype(F32), 1, 2),
        "w_exp_gate": w_exp_gate.astype(BF16),
        "w_exp_up": w_exp_up.astype(BF16),
        "w_exp_down": w_exp_down.astype(BF16),
        "final_g": final_g.astype(F32).reshape(1, D_MODEL),
    }
    lbs = _lower_bounds(b_lb_logits)
    return (_trunk(x_prompt.astype(F32), lbs, p), _trunk(x_sample.astype(F32), lbs, p))
```

```python
import functools

import jax
import jax.numpy as jnp
from jax import lax
from jax.experimental import pallas as pl
from jax.experimental.pallas import tpu as pltpu

F32 = jnp.float32
BF16 = jnp.bfloat16
I32 = jnp.int32

D_MODEL = 1024
N_SPLIT = 9
HEAD_DIM = 128
N_HEADS = 8
SP_CHUNK = 128
SP_GROUPS = 8
N_EXPERTS = 16
CAPACITY_FACTOR = 2
EPS = 1e-6

LANES = 128
HG_CHUNK = 128
HG_UNSAFE = -150.0
MIX_SUB = 512
TOK_CHUNK = 128
ROW_ALIGN = 16
WIN_FAST = 48
WIN_BIG = TOK_CHUNK + ROW_ALIGN
XG_COLS = D_MODEL + LANES
VMEM_LIMIT = 56 * 1024 * 1024


def _cparams(sem):
    return pltpu.CompilerParams(dimension_semantics=sem, vmem_limit_bytes=VMEM_LIMIT)


def _sigmoid(x):
    return 1.0 / (1.0 + jnp.exp(-x))


def _gelu_tanh(x):
    return x * (0.5 * (1.0 + jnp.tanh(0.7978845608028654 * (x + 0.044715 * (x * x * x)))))


def _rmsnorm_rows(x, g):
    ms = jnp.mean(x * x, axis=-1, keepdims=True)
    return x * lax.rsqrt(ms + EPS) * g


def _lb_body(logit_ref, o_ref):
    x = logit_ref[...]
    m = jnp.max(x, axis=1, keepdims=True)
    e = jnp.exp(x - m)
    p = e / jnp.sum(e, axis=1, keepdims=True)
    depth = x.shape[1]
    acc = jnp.zeros_like(p[:, 0:1, :])
    for l in range(depth):
        o_ref[:, l:l + 1, :] = acc
        if l + 1 < depth:
            acc = acc + p[:, l + 1:l + 2, :]


def _lower_bounds(lb_logits):
    return pl.pallas_call(
        _lb_body, out_shape=jax.ShapeDtypeStruct(lb_logits.shape, F32), name="lower_bounds",
    )(lb_logits.astype(F32))


def _inproj_body(x_ref, g1_ref, w_ref, vg_ref, o_ref, xn_ref):
    j = pl.program_id(1)

    @pl.when(j == 0)
    def _():
        xn_ref[...] = _rmsnorm_rows(x_ref[...], g1_ref[...]).astype(BF16)

    def project():
        return jnp.dot(xn_ref[...], w_ref[...], preferred_element_type=F32)

    @pl.when(j == 0)
    def _():
        o_ref[...] = _gelu_tanh(project()).astype(BF16)

    @pl.when(j == 1)
    def _():
        o_ref[...] = _rmsnorm_rows(_gelu_tanh(project()), vg_ref[...]).astype(BF16)

    @pl.when(jnp.logical_and(j >= 2, j <= 5))
    def _():
        o_ref[...] = project().astype(BF16)

    @pl.when(j == 6)
    def _():
        z = project()
        o_ref[...] = (z * _sigmoid(z)).astype(BF16)

    @pl.when(j >= 7)
    def _():
        o_ref[...] = _sigmoid(project()).astype(BF16)


def _inproj(x, g1, w_in, vg):
    T = x.shape[0]
    tm = min(1024, T)
    return pl.pallas_call(
        _inproj_body,
        grid=(T // tm, N_SPLIT),
        in_specs=[
            pl.BlockSpec((tm, D_MODEL), lambda i, j: (i, 0)),
            pl.BlockSpec((1, D_MODEL), lambda i, j: (0, 0)),
            pl.BlockSpec((D_MODEL, D_MODEL), lambda i, j: (0, j)),
            pl.BlockSpec((1, D_MODEL), lambda i, j: (0, 0)),
        ],
        out_specs=pl.BlockSpec((tm, D_MODEL), lambda i, j: (i, j)),
        out_shape=jax.ShapeDtypeStruct((T, N_SPLIT * D_MODEL), BF16),
        scratch_shapes=[pltpu.VMEM((tm, D_MODEL), BF16)],
        compiler_params=_cparams(("parallel", "arbitrary")),
        name="inproj",
    )(x, g1, w_in, vg)


def _hgrn_body(q_ref, v_ref, zf_ref, zb_ref, go_ref, lb_ref, og_ref, o_ref,
               gf_ref, gb_ref, acc_ref, gs_ref, ks_ref, vs_ref, *, L):
    C = HG_CHUNK
    n = L // C
    rid = lax.broadcasted_iota(I32, (C, C), 0)
    cid = lax.broadcasted_iota(I32, (C, C), 1)
    rowi = lax.broadcasted_iota(I32, (C, 1), 0)
    nt = (((1,), (1,)), ((), ()))

    def chunk_slice(c):
        return pl.ds(pl.multiple_of(c * C, C), C)

    class Dir:
        def __init__(self, z_ref, g_ref, lbrow, reverse):
            self.z_ref, self.g_ref, self.reverse = z_ref, g_ref, reverse
            lb = lb_ref[lbrow:lbrow + 1, :]
            self.log_lb = jnp.log(lb)
            self.log1m = jnp.log1p(-lb)
            self.onem = 1.0 - lb
            self.mask = (cid >= rid) if reverse else (cid <= rid)
            self.tri = jnp.where(self.mask, 1.0, 0.0).astype(BF16)

        def total(self, G):
            return G[0:1, :] if self.reverse else G[C - 1:C, :]

    dirs = (Dir(zf_ref, gf_ref, 0, False), Dir(zb_ref, gb_ref, 1, True))

    def gates(c, mn):
        sl = chunk_slice(c)
        for d in dirs:
            z = d.z_ref[sl, :].astype(F32)
            ls = jnp.minimum(z, 0.0) - jnp.log(1.0 + jnp.exp(-jnp.abs(z)))
            b = d.log1m + ls
            g = jnp.maximum(d.log_lb, b) + jnp.log(1.0 + jnp.exp(-jnp.abs(d.log_lb - b)))
            g_hi = g.astype(BF16)
            g_lo = (g - g_hi.astype(F32)).astype(BF16)
            G = (jnp.dot(d.tri, g_hi, preferred_element_type=F32)
                 + jnp.dot(d.tri, g_lo, preferred_element_type=F32))
            d.g_ref[sl, :] = G
            mn = jnp.minimum(mn, d.total(G))
        acc_ref[sl, :] = jnp.zeros((C, HEAD_DIM), F32)
        return mn

    mn = lax.fori_loop(0, n, gates, jnp.zeros((1, HEAD_DIM), F32), unroll=2)
    unsafe = jnp.min(mn) < HG_UNSAFE

    def chunk(d, c, st, exact):
        sl = chunk_slice(c)
        z = d.z_ref[sl, :].astype(F32)
        q = q_ref[sl, :].astype(F32)
        vb = v_ref[sl, :]
        G = d.g_ref[sl, :]
        gtot = d.total(G)
        kk = d.onem * (1.0 / (1.0 + jnp.exp(z)))
        qi = (q * jnp.exp(G)).astype(BF16)
        o = lax.dot_general(qi, st.astype(BF16), nt, preferred_element_type=F32)
        if exact:
            gs_ref[...] = G
            ks_ref[...] = kk
            vs_ref[...] = vb.astype(F32)

            def sbody(s, acc):
                gs = gs_ref[pl.ds(s, 1), :]
                dec = jnp.exp(jnp.minimum(G - gs, 0.0))
                w = jnp.sum(q * ks_ref[pl.ds(s, 1), :] * dec, axis=1, keepdims=True)
                keep = (rowi <= s) if d.reverse else (rowi >= s)
                return acc + jnp.where(keep, w, 0.0) * vs_ref[pl.ds(s, 1), :]

            o = o + lax.fori_loop(0, C, sbody, jnp.zeros((C, HEAD_DIM), F32))
        else:
            gref = 0.5 * gtot
            qa = (q * jnp.exp(G - gref)).astype(BF16)
            ka = (kk * jnp.exp(gref - G)).astype(BF16)
            a = lax.dot_general(qa, ka, nt, preferred_element_type=F32)
            a = jnp.where(d.mask, a, 0.0).astype(BF16)
            o = o + jnp.dot(a, vb, preferred_element_type=F32)
        acc_ref[sl, :] += o
        kst = (kk * jnp.exp(gtot - G)).astype(BF16)
        vt = vb.astype(F32).T.astype(BF16)
        return jnp.exp(gtot) * st + jnp.dot(vt, kst, preferred_element_type=F32)

    def recurrence(exact, unroll):
        def body(ci, carry):
            stf, stb = carry
            return (chunk(dirs[0], ci, stf, exact), chunk(dirs[1], n - 1 - ci, stb, exact))
        zero = jnp.zeros((HEAD_DIM, HEAD_DIM), F32)
        lax.fori_loop(0, n, body, (zero, zero), unroll=unroll)

    @pl.when(unsafe)
    def _():
        recurrence(True, 1)

    @pl.when(jnp.logical_not(unsafe))
    def _():
        recurrence(False, 2)

    og = og_ref[...]

    def finish(c, carry):
        sl = chunk_slice(c)
        tot = acc_ref[sl, :]
        ms = jnp.mean(tot * tot, axis=-1, keepdims=True)
        y = tot * lax.rsqrt(ms + EPS) * og
        o_ref[sl, :] = (y * go_ref[sl, :].astype(F32)).astype(o_ref.dtype)
        return carry

    lax.fori_loop(0, n, finish, 0, unroll=2)


def _hgrn(z3, lb, og):
    B, L, _ = z3.shape
    cb = D_MODEL // HEAD_DIM

    def col(split):
        return pl.BlockSpec((None, L, HEAD_DIM), lambda b, h: (b, 0, split * cb + h))

    return pl.pallas_call(
        functools.partial(_hgrn_body, L=L),
        grid=(B, N_HEADS),
        in_specs=[col(2), col(3), col(4), col(5), col(6),
                  pl.BlockSpec((2, HEAD_DIM), lambda b, h: (0, h)),
                  pl.BlockSpec((1, HEAD_DIM), lambda b, h: (0, 0))],
        out_specs=pl.BlockSpec((None, L, HEAD_DIM), lambda b, h: (b, 0, h)),
        out_shape=jax.ShapeDtypeStruct((B, L, D_MODEL), BF16),
        scratch_shapes=[pltpu.VMEM((L, HEAD_DIM), F32),
                        pltpu.VMEM((L, HEAD_DIM), F32),
                        pltpu.VMEM((L, HEAD_DIM), F32),
                        pltpu.VMEM((HG_CHUNK, HEAD_DIM), F32),
                        pltpu.VMEM((HG_CHUNK, HEAD_DIM), F32),
                        pltpu.VMEM((HG_CHUNK, HEAD_DIM), F32)],
        compiler_params=_cparams(("parallel", "parallel")),
        name="hgrn",
    )(z3, z3, z3, z3, z3, lb, og)


def _mix_body(u_ref, vn_ref, ga_ref, gb_ref, yb_ref, x_ref, ws_ref, bs_ref, wa_ref, wb_ref,
              wo_ref, g2_ref, wrt_ref, xo_ref, xg_ref, afft_ref, ya_ref, *, tm):
    gw = D_MODEL // SP_GROUPS
    sub = min(MIX_SUB, tm)
    for r0 in range(0, tm, sub):
        rows = slice(r0, r0 + sub)
        for ci in range(sub // SP_CHUNK):
            rs = slice(r0 + ci * SP_CHUNK, r0 + (ci + 1) * SP_CHUNK)
            for g in range(SP_GROUPS):
                cs = slice(g * gw, (g + 1) * gw)
                s = (jnp.dot(ws_ref[g], vn_ref[rs, cs], preferred_element_type=F32)
                     + bs_ref[:, g:g + 1])
                ya_ref[rs, cs] = (u_ref[rs, cs].astype(F32) * s).astype(BF16)
        pa = jnp.dot(ya_ref[rows, :], wa_ref[...], preferred_element_type=F32)
        pb = jnp.dot(yb_ref[rows, :], wb_ref[...], preferred_element_type=F32)
        m = ga_ref[rows, :].astype(F32) * pa + gb_ref[rows, :].astype(F32) * pb
        xo = x_ref[rows, :] + jnp.dot(m.astype(BF16), wo_ref[...], preferred_element_type=F32)
        xo_ref[rows, :] = xo
        xn = _rmsnorm_rows(xo, g2_ref[...])
        lgt = lax.dot_general(wrt_ref[...], xn, (((1,), (1,)), ((), ())),
                              precision=lax.Precision.HIGHEST, preferred_element_type=F32)
        e2 = jnp.exp(lgt - jnp.max(lgt, axis=0, keepdims=True))
        afft = e2 / jnp.sum(e2, axis=0, keepdims=True)
        afft_ref[:, rows] = afft
        p0 = afft.astype(BF16).astype(F32)
        r1 = afft - p0
        p1 = r1.astype(BF16).astype(F32)
        p2 = r1 - p1
        pad = jnp.zeros((LANES - 3 * N_EXPERTS, sub), F32)
        pieces = jnp.concatenate([p0, p1, p2, pad], axis=0).T
        xg_ref[rows, :D_MODEL] = xn.astype(BF16)
        xg_ref[rows, D_MODEL:] = pieces.astype(BF16)


def _mix(z, yb, x, ws, bs_t, wa, wb, wo, g2, wrt):
    T = x.shape[0]
    tm = min(512, T)
    row = lambda j: pl.BlockSpec((tm, D_MODEL), lambda i, j=j: (i, j))
    full = lambda shp: pl.BlockSpec(shp, lambda i: (0,) * len(shp))
    return pl.pallas_call(
        functools.partial(_mix_body, tm=tm),
        grid=(T // tm,),
        in_specs=[row(0), row(1), row(7), row(8), row(0), row(0),
                  full((SP_GROUPS, SP_CHUNK, SP_CHUNK)), full((SP_CHUNK, SP_GROUPS)),
                  full((D_MODEL, D_MODEL)), full((D_MODEL, D_MODEL)), full((D_MODEL, D_MODEL)),
                  full((1, D_MODEL)), full((N_EXPERTS, D_MODEL))],
        out_specs=[pl.BlockSpec((tm, D_MODEL), lambda i: (i, 0)),
                   pl.BlockSpec((tm, XG_COLS), lambda i: (i, 0)),
                   pl.BlockSpec((N_EXPERTS, tm), lambda i: (0, i))],
        out_shape=[jax.ShapeDtypeStruct((T, D_MODEL), F32),
                   jax.ShapeDtypeStruct((T, XG_COLS), BF16),
                   jax.ShapeDtypeStruct((N_EXPERTS, T), F32)],
        scratch_shapes=[pltpu.VMEM((tm, D_MODEL), BF16)],
        compiler_params=_cparams(("parallel",)),
        name="mix",
    )(z, z, z, z, yb, x, ws, bs_t, wa, wb, wo, g2, wrt)


def _select_body(aff_ref, posm_ref, offs_ref, run_ref, *, T, cap):
    E = N_EXPERTS
    CH = TOK_CHUNK
    nc = T // CH
    ncp = offs_ref.shape[1]
    BL = min(T, 2048)
    nb = T // BL

    def bits(sl):
        return lax.bitcast_convert_type(aff_ref[:, sl], I32)

    def count_ge(cand):
        def blk(i, acc):
            b = bits(pl.ds(pl.multiple_of(i * BL, BL), BL))
            return acc + jnp.where(b >= cand, 1, 0).astype(I32)
        acc = lax.fori_loop(0, nb, blk, jnp.zeros((E, BL), I32))
        return jnp.sum(acc, axis=1, keepdims=True)

    def rbody(i, prefix):
        cand = prefix | lax.shift_left(jnp.int32(1), 30 - i)
        return jnp.where(count_ge(cand) >= cap, cand, prefix)

    thr = lax.fori_loop(0, 31, rbody, jnp.zeros((E, 1), I32))
    need = (cap - count_ge(thr + 1)).astype(F32)

    r = lax.broadcasted_iota(I32, (CH, CH), 0)
    c = lax.broadcasted_iota(I32, (CH, CH), 1)
    upper = jnp.where(r <= c, 1.0, 0.0).astype(BF16)
    lane = lax.broadcasted_iota(I32, (E, ncp), 1)

    run_ref[...] = jnp.zeros_like(run_ref)
    offs_ref[...] = jnp.zeros_like(offs_ref)

    def cbody(ci, carry):
        eq_run = run_ref[0][:, 0:1]
        sel_run = run_ref[1][:, 0:1]
        sl = pl.ds(pl.multiple_of(ci * CH, CH), CH)
        b = bits(sl)
        gt = b > thr
        eq = b == thr
        eqf = jnp.where(eq, 1.0, 0.0)
        eqcs = jnp.dot(eqf.astype(BF16), upper, preferred_element_type=F32)
        sel = jnp.logical_or(gt, jnp.logical_and(eq, (eq_run + eqcs - eqf) < need))
        self_ = jnp.where(sel, 1.0, 0.0)
        selcs = jnp.dot(self_.astype(BF16), upper, preferred_element_type=F32)
        pos = sel_run + selcs - self_
        posm_ref[:, sl] = jnp.where(sel, pos, -1.0).astype(I32)
        offs_ref[...] = jnp.where(lane == ci, sel_run.astype(I32), offs_ref[...])
        run_ref[0] = jnp.broadcast_to(eq_run + eqcs[:, CH - 1:CH], (E, LANES))
        run_ref[1] = jnp.broadcast_to(sel_run + selcs[:, CH - 1:CH], (E, LANES))
        return carry

    lax.fori_loop(0, nc, cbody, 0)
    offs_ref[...] = jnp.where(lane == nc, run_ref[1][:, 0:1].astype(I32), offs_ref[...])


def _select(afft, cap):
    E, T = afft.shape
    nc = T // TOK_CHUNK
    ncp = ((nc + 1 + LANES - 1) // LANES) * LANES
    return pl.pallas_call(
        functools.partial(_select_body, T=T, cap=cap),
        out_shape=[jax.ShapeDtypeStruct((E, T), I32), jax.ShapeDtypeStruct((E, ncp), I32)],
        scratch_shapes=[pltpu.VMEM((2, E, LANES), F32)],
        compiler_params=pltpu.CompilerParams(vmem_limit_bytes=VMEM_LIMIT),
        name="select",
    )(afft)


def _chunk_scalars(offs_ref, c):
    off = [offs_ref[e, c] for e in range(N_EXPERTS)]
    cnt = [offs_ref[e, c + 1] - off[e] for e in range(N_EXPERTS)]
    nmax = cnt[0]
    for e in range(1, N_EXPERTS):
        nmax = jnp.maximum(nmax, cnt[e])
    return off, cnt, nmax <= WIN_FAST - ROW_ALIGN


def _column(vals):
    rowid = lax.broadcasted_iota(I32, (N_EXPERTS, 1), 0)
    col = jnp.zeros((N_EXPERTS, 1), I32)
    for e, v in enumerate(vals):
        col = jnp.where(rowid == e, v, col)
    return col


def _align_down(v):
    return v & (-ROW_ALIGN)


def _gather_body(offs_ref, xg_ref, posm_ref, xe_ref, carry_ref, mer_ref, stage_ref, big_ref,
                 pend_ref, sem, *, nc, cap):
    E = N_EXPERTS
    c = pl.program_id(0)
    par = c & 1

    @pl.when(c == 0)
    def _():
        carry_ref[...] = jnp.zeros_like(carry_ref)
        pend_ref[0] = 0
        big_ref[0] = jnp.zeros((WIN_BIG, XG_COLS), BF16)
        for e in range(E):
            pltpu.make_async_copy(big_ref.at[0], xe_ref.at[e, pl.ds(cap, WIN_BIG), :], sem.at[2]).start()
        for e in range(E):
            pltpu.make_async_copy(big_ref.at[0], xe_ref.at[e, pl.ds(cap, WIN_BIG), :], sem.at[2]).wait()

    off, cnt, fast = _chunk_scalars(offs_ref, c)
    sa = [_align_down(off[e]) for e in range(E)]
    delta = [_align_down(off[e] + cnt[e]) - sa[e] for e in range(E)]
    posm = posm_ref[...]
    rr = jnp.where(posm >= 0, posm - _column(sa), -1)

    def onehot(e, rows):
        kio = lax.broadcasted_iota(I32, (rows, TOK_CHUNK), 0)
        return jnp.where(kio == rr[e:e + 1, :], 1.0, 0.0).astype(BF16)

    def merge(e, reg, rows, dst):
        mer_ref[0:ROW_ALIGN, :] = reg[0:ROW_ALIGN, :] + carry_ref[e]
        mer_ref[ROW_ALIGN:rows, :] = reg[ROW_ALIGN:rows, :]
        dst[...] = mer_ref[0:rows, :].astype(BF16)
        carry_ref[e] = mer_ref[pl.ds(pl.multiple_of(delta[e], ROW_ALIGN), ROW_ALIGN), :]

    def window(e, rows):
        return xe_ref.at[e, pl.ds(pl.multiple_of(sa[e], ROW_ALIGN), rows), :]

    def wait_pending():
        @pl.when(pend_ref[0] == 1)
        def _():
            for e in range(E):
                pltpu.make_async_copy(stage_ref.at[1 - par, e], window(e, WIN_FAST),
                                      sem.at[1 - par]).wait()
            pend_ref[0] = 0

    @pl.when(fast)
    def _():
        p = jnp.concatenate([onehot(e, WIN_FAST) for e in range(E)], axis=0)
        res = jnp.dot(p, xg_ref[...], preferred_element_type=F32)
        for e in range(E):
            merge(e, res[e * WIN_FAST:(e + 1) * WIN_FAST, :], WIN_FAST, stage_ref.at[par, e])
        wait_pending()
        for e in range(E):
            pltpu.make_async_copy(stage_ref.at[par, e], window(e, WIN_FAST), sem.at[par]).start()
        pend_ref[0] = 1

    @pl.when(jnp.logical_not(fast))
    def _():
        wait_pending()
        for e in range(E):
            res = jnp.dot(onehot(e, WIN_BIG), xg_ref[...], preferred_element_type=F32)
            merge(e, res, WIN_BIG, big_ref.at[e])
        for e in range(E):
            pltpu.make_async_copy(big_ref.at[e], window(e, WIN_BIG), sem.at[2]).start()
        for e in range(E):
            pltpu.make_async_copy(big_ref.at[e], window(e, WIN_BIG), sem.at[2]).wait()

    @pl.when(c == nc - 1)
    def _():
        @pl.when(pend_ref[0] == 1)
        def _():
            for e in range(E):
                pltpu.make_async_copy(stage_ref.at[par, e], window(e, WIN_FAST), sem.at[par]).wait()
            pend_ref[0] = 0


def _gather(offs, xg, posm, cap):
    T = xg.shape[0]
    nc = T // TOK_CHUNK
    E = N_EXPERTS
    grid_spec = pltpu.PrefetchScalarGridSpec(
        num_scalar_prefetch=1,
        grid=(nc,),
        in_specs=[pl.BlockSpec((TOK_CHUNK, XG_COLS), lambda c, o: (c, 0)),
                  pl.BlockSpec((E, TOK_CHUNK), lambda c, o: (0, c))],
        out_specs=pl.BlockSpec(memory_space=pl.ANY),
        scratch_shapes=[pltpu.VMEM((E, ROW_ALIGN, XG_COLS), F32),
                        pltpu.VMEM((WIN_BIG, XG_COLS), F32),
                        pltpu.VMEM((2, E, WIN_FAST, XG_COLS), BF16),
                        pltpu.VMEM((E, WIN_BIG, XG_COLS), BF16),
                        pltpu.SMEM((1,), I32),
                        pltpu.SemaphoreType.DMA((3,))],
    )
    return pl.pallas_call(
        functools.partial(_gather_body, nc=nc, cap=cap),
        grid_spec=grid_spec,
        out_shape=jax.ShapeDtypeStruct((E, cap + WIN_BIG, XG_COLS), BF16),
        compiler_params=_cparams(("arbitrary",)),
        name="gather",
    )(offs, xg, posm)


def _ffn_body(xe_ref, wg_ref, wu_ref, wd_ref, o_ref):
    e = pl.program_id(0)
    x = xe_ref[:, :D_MODEL]
    pieces = xe_ref[:, D_MODEL:].astype(F32)
    lane = lax.broadcasted_iota(I32, pieces.shape, 1)
    mine = jnp.logical_and((lane & (N_EXPERTS - 1)) == e, lane < 3 * N_EXPERTS)
    gate = jnp.sum(jnp.where(mine, pieces, 0.0), axis=1, keepdims=True)
    hg = jnp.dot(x, wg_ref[...], preferred_element_type=F32)
    hu = jnp.dot(x, wu_ref[...], preferred_element_type=F32)
    h = (hg * _sigmoid(hg) * hu).astype(BF16)
    y = jnp.dot(h, wd_ref[...], preferred_element_type=F32)
    o_ref[...] = (y * gate).astype(o_ref.dtype)


def _ffn(xe, wg, wu, wd, cap):
    E = N_EXPERTS
    tm = min(512, cap)
    ff = wg.shape[-1]
    return pl.pallas_call(
        _ffn_body,
        grid=(E, cap // tm),
        in_specs=[pl.BlockSpec((None, tm, XG_COLS), lambda e, i: (e, i, 0)),
                  pl.BlockSpec((None, D_MODEL, ff), lambda e, i: (e, 0, 0)),
                  pl.BlockSpec((None, D_MODEL, ff), lambda e, i: (e, 0, 0)),
                  pl.BlockSpec((None, ff, D_MODEL), lambda e, i: (e, 0, 0))],
        out_specs=pl.BlockSpec((None, tm, D_MODEL), lambda e, i: (e, i, 0)),
        out_shape=jax.ShapeDtypeStruct((E, cap, D_MODEL), BF16),
        compiler_params=_cparams(("parallel", "parallel")),
        name="ffn",
    )(xe, wg, wu, wd)


def _combine_body(offs_ref, posm_ref, x_ref, fg_ref, ye_ref, o_ref, ybuf_ref, ybig_ref, sem,
                  *, nc, cap, final_norm):
    E = N_EXPERTS
    c = pl.program_id(0)

    def starts(cc, rows):
        off, _, fast = _chunk_scalars(offs_ref, cc)
        return [jnp.minimum(_align_down(off[e]), cap - rows) for e in range(E)], fast

    def window(e, start, rows):
        return ye_ref.at[e, pl.ds(pl.multiple_of(start, ROW_ALIGN), rows), :]

    def fetch(cc):
        sa, fast = starts(cc, WIN_FAST)

        @pl.when(fast)
        def _():
            slot = cc & 1
            for e in range(E):
                pltpu.make_async_copy(window(e, sa[e], WIN_FAST),
                                      ybuf_ref.at[slot, pl.ds(e * WIN_FAST, WIN_FAST), :],
                                      sem.at[slot]).start()

    @pl.when(c == 0)
    def _():
        fetch(c)

    @pl.when(c + 1 < nc)
    def _():
        fetch(c + 1)

    posm = posm_ref[...]

    def unpack(sa, rows, y):
        rr = jnp.where(posm >= 0, posm - _column(sa), -1).astype(F32)
        rt = rr.T.astype(BF16)
        ecol = lax.broadcasted_iota(I32, (E, E * rows), 1) // rows
        erow = lax.broadcasted_iota(I32, (E, E * rows), 0)
        spread = jnp.where(ecol == erow, 1.0, 0.0).astype(BF16)
        rexp = jnp.dot(rt, spread, preferred_element_type=F32)
        jmod = (lax.broadcasted_iota(I32, (1, E * rows), 1) % rows).astype(F32)
        w = jnp.where(rexp == jmod, 1.0, 0.0).astype(BF16)
        return jnp.dot(w, y, preferred_element_type=F32)

    def finish(upd):
        xo = x_ref[...] + upd
        if final_norm:
            xo = _rmsnorm_rows(xo, fg_ref[...])
        o_ref[...] = xo

    sa_f, fast = starts(c, WIN_FAST)

    @pl.when(fast)
    def _():
        slot = c & 1
        for e in range(E):
            pltpu.make_async_copy(window(e, sa_f[e], WIN_FAST),
                                  ybuf_ref.at[slot, pl.ds(e * WIN_FAST, WIN_FAST), :],
                                  sem.at[slot]).wait()
        finish(unpack(sa_f, WIN_FAST, ybuf_ref[slot]))

    @pl.when(jnp.logical_not(fast))
    def _():
        sa_b, _ = starts(c, WIN_BIG)
        for e in range(E):
            pltpu.make_async_copy(window(e, sa_b[e], WIN_BIG),
                                  ybig_ref.at[pl.ds(e * WIN_BIG, WIN_BIG), :], sem.at[2]).start()
        for e in range(E):
            pltpu.make_async_copy(window(e, sa_b[e], WIN_BIG),
                                  ybig_ref.at[pl.ds(e * WIN_BIG, WIN_BIG), :], sem.at[2]).wait()
        finish(unpack(sa_b, WIN_BIG, ybig_ref[...]))


def _combine(offs, posm, x, fg, ye, cap, final_norm):
    T = x.shape[0]
    nc = T // TOK_CHUNK
    E = N_EXPERTS
    grid_spec = pltpu.PrefetchScalarGridSpec(
        num_scalar_prefetch=1,
        grid=(nc,),
        in_specs=[pl.BlockSpec((E, TOK_CHUNK), lambda c, o: (0, c)),
                  pl.BlockSpec((TOK_CHUNK, D_MODEL), lambda c, o: (c, 0)),
                  pl.BlockSpec((1, D_MODEL), lambda c, o: (0, 0)),
                  pl.BlockSpec(memory_space=pl.ANY)],
        out_specs=pl.BlockSpec((TOK_CHUNK, D_MODEL), lambda c, o: (c, 0)),
        scratch_shapes=[pltpu.VMEM((2, E * WIN_FAST, D_MODEL), BF16),
                        pltpu.VMEM((E * WIN_BIG, D_MODEL), BF16),
                        pltpu.SemaphoreType.DMA((3,))],
    )
    return pl.pallas_call(
        functools.partial(_combine_body, nc=nc, cap=cap, final_norm=final_norm),
        grid_spec=grid_spec,
        out_shape=jax.ShapeDtypeStruct((T, D_MODEL), F32),
        compiler_params=_cparams(("arbitrary",)),
        name="combine",
    )(offs, posm, x, fg, ye)


def _trunk(x, lbs, p):
    B, L, _ = x.shape
    T = B * L
    cap = CAPACITY_FACTOR * T // N_EXPERTS
    depth = p["w_in"].shape[0]
    assert L % HG_CHUNK == 0 and T % TOK_CHUNK == 0 and cap % ROW_ALIGN == 0 and cap >= WIN_BIG
    x = x.reshape(T, D_MODEL)
    for l in range(depth):
        z = _inproj(x, p["norm1_g"][l], p["w_in"][l], p["a_vnorm_g"][l])
        yb = _hgrn(z.reshape(B, L, N_SPLIT * D_MODEL), lbs[:, l], p["b_onorm_g"][l])
        x, xg, afft = _mix(z, yb.reshape(T, D_MODEL), x, p["a_spatial_w"][l], p["a_spatial_b_t"][l],
                           p["w_branch_a"][l], p["w_branch_b"][l], p["w_out"][l], p["norm2_g"][l],
                           p["w_router_t"][l])
        posm, offs = _select(afft, cap)
        xe = _gather(offs, xg, posm, cap)
        ye = _ffn(xe, p["w_exp_gate"][l], p["w_exp_up"][l], p["w_exp_down"][l], cap)
        x = _combine(offs, posm, x, p["final_g"], ye, cap, final_norm=(l == depth - 1))
    return x.reshape(B, L, D_MODEL)


def kernel(x_prompt, x_sample, norm1_g, w_in, a_vnorm_g, a_spatial_w, a_spatial_b, b_lb_logits,
           b_onorm_g, w_branch_a, w_branch_b, w_out, norm2_g, w_router, w_exp_gate, w_exp_up,
           w_exp_down, final_g):
    depth = w_in.shape[0]
    p = {
        "norm1_g": norm1_g.astype(F32).reshape(depth, 1, D_MODEL),
        "w_in": w_in.astype(BF16),
        "a_vnorm_g": a_vnorm_g.astype(F32).reshape(depth, 1, D_MODEL),
        "a_spatial_w": a_spatial_w.astype(BF16),
        "a_spatial_b_t": jnp.swapaxes(a_spatial_b.astype(F32), 1, 2),
        "b_onorm_g": b_onorm_g.astype(F32).reshape(depth, 1, HEAD_DIM),
        "w_branch_a": w_branch_a.astype(BF16),
        "w_branch_b": w_branch_b.astype(BF16),
        "w_out": w_out.astype(BF16),
        "norm2_g": norm2_g.astype(F32).reshape(depth, 1, D_MODEL),
        "w_router_t": jnp.swapaxes(w_router.astype(F32), 1, 2),
        "w_exp_gate": w_exp_gate.astype(BF16),
        "w_exp_up": w_exp_up.astype(BF16),
        "w_exp_down": w_exp_down.astype(BF16),
        "final_g": final_g.astype(F32).reshape(1, D_MODEL),
    }
    lbs = _lower_bounds(b_lb_logits)
    return (_trunk(x_prompt.astype(F32), lbs, p), _trunk(x_sample.astype(F32), lbs, p))
```
